```python
import jax
import jax.numpy as jnp
from jax import lax
import numpy as np

D_MODEL = 1024
BATCH = 8
SEQ = 8192
DEPTH = 1

NORM_EPS = 1e-6
D_FF = 2816

ATTN_HEAD_DIM = 64
ATTN_HEADS_PER_GROUP = 4
ATTN_GROUPS = ((128, 1), (512, 4), (2048, 16))
N_ATTN_GROUPS = 3
ATTN_WIDTH = N_ATTN_GROUPS * ATTN_HEADS_PER_GROUP * ATTN_HEAD_DIM
ATTN_OUT_WIDTH = ATTN_HEADS_PER_GROUP * ATTN_HEAD_DIM
ROPE_THETA = 500000.0
ROPE_DIM = ATTN_HEAD_DIM // 4

RWKV_HEAD_DIM = 64
RWKV_WIDTH = D_MODEL
RWKV_HEADS = RWKV_WIDTH // RWKV_HEAD_DIM
DECAY_LORA = 64
ICLR_LORA = 64
GATE_LORA = 160
RWKV_GN_EPS = 64e-5
RWKV_STREAM = 3 * RWKV_WIDTH + DECAY_LORA + ICLR_LORA + GATE_LORA

IN_COLS = 3 * ATTN_WIDTH + RWKV_STREAM + 2 * D_MODEL

kernel_name = 'hybrid_dilated_attn_rwkv7_macaron'


def rms_norm(x, gain):
    xf = x.astype(jnp.float32)
    y = xf * lax.rsqrt(jnp.mean(xf * xf, axis=-1, keepdims=True) + NORM_EPS)
    return (y * gain.astype(jnp.float32)).astype(x.dtype)


def swiglu(h, w_gate, w_up, w_down):
    return (jax.nn.silu(h @ w_gate) * (h @ w_up)) @ w_down


def partial_rope(x, positions):
    half = ROPE_DIM // 2
    inv_freq = jnp.power(ROPE_THETA, -jnp.arange(half, dtype=jnp.float32) * (2.0 / ROPE_DIM))
    ang = positions.astype(jnp.float32)[:, None] * inv_freq[None, :]
    cos = jnp.cos(ang)[None, :, None, :]
    sin = jnp.sin(ang)[None, :, None, :]
    xr = x[..., :ROPE_DIM].astype(jnp.float32)
    x1, x2 = xr[..., :half], xr[..., half:]
    rot = jnp.concatenate([x1 * cos - x2 * sin, x2 * cos + x1 * sin], axis=-1).astype(x.dtype)
    return jnp.concatenate([rot, x[..., ROPE_DIM:]], axis=-1)


def dilated_group(q, k, v, window, dilation):
    B, S, H, Dh = q.shape
    n = window // dilation
    L = S // dilation
    nb = -(-L // n)
    Lp = nb * n

    def to_sub(t):
        t = t.reshape(B, L, dilation, H, Dh).transpose(0, 2, 3, 1, 4)
        t = jnp.pad(t, ((0, 0), (0, 0), (0, 0), (0, Lp - L), (0, 0)))
        return t.reshape(B, dilation, H, nb, n, Dh)

    def with_prev(t):
        prev = jnp.pad(t, ((0, 0), (0, 0), (0, 0), (1, 0), (0, 0), (0, 0)))[:, :, :, :-1]
        return jnp.concatenate([prev, t], axis=4)

    def from_sub(t):
        t = t.reshape(B, dilation, H, Lp, t.shape[-1])[:, :, :, :L]
        return t.transpose(0, 3, 1, 2, 4).reshape(B, S, H, t.shape[-1])

    qb = to_sub(q)
    kw = with_prev(to_sub(k))
    vw = with_prev(to_sub(v))
    s = jnp.einsum('brhiqd,brhikd->brhiqk', qb, kw, preferred_element_type=jnp.float32)
    qi = jnp.arange(n)[:, None]
    ki = jnp.arange(2 * n)[None, :]
    rel = qi + n - ki
    band = (rel >= 0) & (rel <= n)
    blk = jnp.arange(nb)[:, None, None]
    valid = band[None] & ((blk > 0) | (ki[None] >= n))
    s = jnp.where(valid, s, -jnp.inf)
    m = jnp.max(s, axis=-1)
    p = jnp.exp(s - m[..., None])
    l = jnp.sum(p, axis=-1)
    acc = jnp.einsum('brhiqk,brhikd->brhiqd', p, vw.astype(jnp.float32))
    return from_sub(acc), from_sub(m[..., None])[..., 0], from_sub(l[..., None])[..., 0]


def dilated_attention(qkv, positions):
    B, S, _ = qkv.shape
    G, Hg, Dh = N_ATTN_GROUPS, ATTN_HEADS_PER_GROUP, ATTN_HEAD_DIM
    q, k, v = jnp.split(qkv, 3, axis=-1)
    q = partial_rope(q.reshape(B, S, G * Hg, Dh), positions) * (Dh ** -0.5)
    k = partial_rope(k.reshape(B, S, G * Hg, Dh), positions)
    v = v.reshape(B, S, G * Hg, Dh)
    accs, ms, ls = [], [], []
    for g, (window, dilation) in enumerate(ATTN_GROUPS):
        sl = slice(g * Hg, (g + 1) * Hg)
        acc, m, l = dilated_group(q[:, :, sl], k[:, :, sl], v[:, :, sl], window, dilation)
        accs.append(acc)
        ms.append(m)
        ls.append(l)
    m_all = jnp.stack(ms)
    c = jnp.exp(m_all - jnp.max(m_all, axis=0))
    num = jnp.sum(c[..., None] * jnp.stack(accs), axis=0)
    den = jnp.sum(c * jnp.stack(ls), axis=0)
    out = (num / den[..., None]).astype(qkv.dtype)
    return out.reshape(B, S, Hg * Dh)


def token_shift(z, mu):
    prev = jnp.pad(z, ((0, 0), (1, 0), (0, 0)))[:, :-1]
    return z + (prev - z) * mu


def wkv7_scan(r, w, k, v, a, b):
    B, S, H, N = r.shape

    def step(state, inp):
        r_t, w_t, k_t, v_t, a_t, b_t = inp
        sa = jnp.einsum('bhvk,bhk->bhv', state, a_t)
        state = (state * w_t[:, :, None, :] + sa[..., None] * b_t[:, :, None, :]
                 + v_t[..., None] * k_t[:, :, None, :])
        return state, jnp.einsum('bhvk,bhk->bhv', state, r_t)

    xs = tuple(jnp.moveaxis(t.astype(jnp.float32), 1, 0) for t in (r, w, k, v, a, b))
    _, ys = lax.scan(step, jnp.zeros((B, H, N, N), jnp.float32), xs)
    return jnp.moveaxis(ys, 0, 1)


def rwkv7_time_mix(z, w0, w2, a0, a2, g2, k_k, k_a, r_k, ln_w, ln_b, w_out):
    B, S, _ = z.shape
    RW, H, N = RWKV_WIDTH, RWKV_HEADS, RWKV_HEAD_DIM
    o1, o2, o3 = RW, 2 * RW, 3 * RW
    o4, o5 = o3 + DECAY_LORA, o3 + DECAY_LORA + ICLR_LORA
    r, k, v = z[..., :o1], z[..., o1:o2], z[..., o2:o3]
    zw, za, zg = z[..., o3:o4], z[..., o4:o5], z[..., o5:]
    logw = -jax.nn.softplus(-(w0 + jnp.tanh(zw) @ w2)) - 0.5
    decay = jnp.exp(-jnp.exp(logw.astype(jnp.float32)))
    a = jax.nn.sigmoid(a0 + za @ a2)
    g = jax.nn.sigmoid(zg) @ g2
    kk = (k * k_k).astype(jnp.float32).reshape(B, S, H, N)
    kk = kk / jnp.maximum(jnp.sqrt(jnp.sum(kk * kk, axis=-1, keepdims=True)), 1e-12)
    k = k * (1.0 + (a - 1.0) * k_a)
    rh = r.reshape(B, S, H, N).astype(jnp.float32)
    kh = k.reshape(B, S, H, N).astype(jnp.float32)
    vh = v.reshape(B, S, H, N).astype(jnp.float32)
    ah = a.reshape(B, S, H, N).astype(jnp.float32)
    y = wkv7_scan(rh, decay.reshape(B, S, H, N), kh, vh, -kk, kk * ah)
    mu = jnp.mean(y, axis=-1, keepdims=True)
    var = jnp.mean(jnp.square(y - mu), axis=-1, keepdims=True)
    yn = ((y - mu) * lax.rsqrt(var + RWKV_GN_EPS)).reshape(B, S, RW)
    yn = yn * ln_w.astype(jnp.float32) + ln_b.astype(jnp.float32)
    bonus = (jnp.sum(rh * kh * r_k.astype(jnp.float32), axis=-1, keepdims=True) * vh).reshape(B, S, RW)
    return ((yn + bonus) * g.astype(jnp.float32)).astype(z.dtype) @ w_out


def hybrid_mixer(h, positions, w_in, gate_bias, attn_w_up, rwkv_mu, rwkv_w0, rwkv_w2, rwkv_a0,
                 rwkv_a2, rwkv_g2, rwkv_k_k, rwkv_k_a, rwkv_r_k, rwkv_ln_w, rwkv_ln_b,
                 rwkv_w_out, w_o):
    proj = h @ w_in
    aw, rs = 3 * ATTN_WIDTH, RWKV_STREAM
    gates = jax.nn.sigmoid(proj[..., aw + rs:] + gate_bias)
    y_attn = dilated_attention(proj[..., :aw], positions) @ attn_w_up
    y_rwkv = rwkv7_time_mix(token_shift(proj[..., aw:aw + rs], rwkv_mu), rwkv_w0, rwkv_w2,
                            rwkv_a0, rwkv_a2, rwkv_g2, rwkv_k_k, rwkv_k_a, rwkv_r_k,
                            rwkv_ln_w, rwkv_ln_b, rwkv_w_out)
    merged = gates[..., :D_MODEL] * y_attn + gates[..., D_MODEL:] * y_rwkv
    return merged @ w_o


def setup_inputs(seed: int = 0) -> dict:
    key = jax.random.key(seed)
    ks = iter(jax.random.split(key, 40))
    f32 = jnp.float32

    def nrm(shape, scale):
        return jax.random.normal(next(ks), shape, f32) * scale

    def gain(shape):
        return 1.0 + 0.05 * jax.random.normal(next(ks), shape, f32)

    L, D, F = DEPTH, D_MODEL, D_FF
    RW, H, N = RWKV_WIDTH, RWKV_HEADS, RWKV_HEAD_DIM
    return {
        'x': nrm((BATCH, SEQ, D), 1.0),
        'ffn1_norm': gain((L, D)),
        'ffn1_w_gate': nrm((L, D, F), D ** -0.5),
        'ffn1_w_up': nrm((L, D, F), D ** -0.5),
        'ffn1_w_down': nrm((L, F, D), F ** -0.5),
        'mix_norm': gain((L, D)),
        'w_in': nrm((L, D, IN_COLS), D ** -0.5),
        'gate_bias': nrm((L, 2 * D), 0.1),
        'attn_w_up': nrm((L, ATTN_OUT_WIDTH, D), ATTN_OUT_WIDTH ** -0.5),
        'rwkv_mu': jax.random.uniform(next(ks), (L, RWKV_STREAM), f32),
        'rwkv_w0': jax.random.uniform(next(ks), (L, RW), f32, -6.0, -1.0),
        'rwkv_w2': nrm((L, DECAY_LORA, RW), 0.1 * DECAY_LORA ** -0.5),
        'rwkv_a0': nrm((L, RW), 0.1),
        'rwkv_a2': nrm((L, ICLR_LORA, RW), 0.1 * ICLR_LORA ** -0.5),
        'rwkv_g2': nrm((L, GATE_LORA, RW), GATE_LORA ** -0.5),
        'rwkv_k_k': 0.85 + 0.05 * jax.random.normal(next(ks), (L, RW), f32),
        'rwkv_k_a': gain((L, RW)),
        'rwkv_r_k': nrm((L, H, N), 0.1),
        'rwkv_ln_w': gain((L, RW)),
        'rwkv_ln_b': nrm((L, RW), 0.02),
        'rwkv_w_out': nrm((L, RW, D), RW ** -0.5),
        'w_o': nrm((L, D, D), D ** -0.5),
        'ffn2_norm': gain((L, D)),
        'ffn2_w_gate': nrm((L, D, F), D ** -0.5),
        'ffn2_w_up': nrm((L, D, F), D ** -0.5),
        'ffn2_w_down': nrm((L, F, D), F ** -0.5),
        'final_norm': gain((D,)),
    }


def reference(x, ffn1_norm, ffn1_w_gate, ffn1_w_up, ffn1_w_down, mix_norm, w_in, gate_bias,
              attn_w_up, rwkv_mu, rwkv_w0, rwkv_w2, rwkv_a0, rwkv_a2, rwkv_g2, rwkv_k_k,
              rwkv_k_a, rwkv_r_k, rwkv_ln_w, rwkv_ln_b, rwkv_w_out, w_o, ffn2_norm,
              ffn2_w_gate, ffn2_w_up, ffn2_w_down, final_norm):
    positions = jnp.arange(x.shape[1], dtype=jnp.int32)
    for layer in range(DEPTH):
        x = x + 0.5 * swiglu(rms_norm(x, ffn1_norm[layer]), ffn1_w_gate[layer],
                             ffn1_w_up[layer], ffn1_w_down[layer])
        x = x + hybrid_mixer(rms_norm(x, mix_norm[layer]), positions, w_in[layer],
                             gate_bias[layer], attn_w_up[layer], rwkv_mu[layer],
                             rwkv_w0[layer], rwkv_w2[layer], rwkv_a0[layer], rwkv_a2[layer],
                             rwkv_g2[layer], rwkv_k_k[layer], rwkv_k_a[layer],
                             rwkv_r_k[layer], rwkv_ln_w[layer], rwkv_ln_b[layer],
                             rwkv_w_out[layer], w_o[layer])
        x = x + 0.5 * swiglu(rms_norm(x, ffn2_norm[layer]), ffn2_w_gate[layer],
                             ffn2_w_up[layer], ffn2_w_down[layer])
    return rms_norm(x, final_norm)
```

```python
import functools

import jax
import jax.numpy as jnp
from jax import lax
from jax.experimental import pallas as pl
from jax.experimental.pallas import tpu as pltpu

F32 = jnp.float32
BF16 = jnp.bfloat16

D_MODEL = 1024
D_FF = 2816
NORM_EPS = 1e-6

HEAD_DIM = 64
ATTN_HEADS_PER_GROUP = 4
ATTN_GROUPS = ((128, 1), (512, 4), (2048, 16))
ATTN_WIDTH = len(ATTN_GROUPS) * ATTN_HEADS_PER_GROUP * HEAD_DIM
ATTN_OUT_WIDTH = ATTN_HEADS_PER_GROUP * HEAD_DIM
ROPE_THETA = 500000.0
ROPE_DIM = HEAD_DIM // 4

RWKV_WIDTH = D_MODEL
DECAY_LORA = 64
ICLR_LORA = 64
GATE_LORA = 160
LORA_WIDTH = DECAY_LORA + ICLR_LORA + GATE_LORA
RWKV_GN_EPS = 64e-5
RWKV_STREAM = 3 * RWKV_WIDTH + LORA_WIDTH

LANES = 128
VMEM_LIMIT_BYTES = 56 * 1024 * 1024

FFN_ROWS = 512
PROJ_ROWS = 256
MERGE_ROWS = 512
ATTN_ROWS = 512
ATTN_BLOCK = 128
RWKV_CHUNK = 64
RWKV_ROWS = 256


def _params(*semantics):
    return pltpu.CompilerParams(dimension_semantics=semantics,
                                vmem_limit_bytes=VMEM_LIMIT_BYTES)


def _resident(shape):
    nd = len(shape)
    return pl.BlockSpec(shape, lambda *_: (0,) * nd, pipeline_mode=pl.Buffered(1))


def _rms(x, gain):
    return x * lax.rsqrt(jnp.mean(x * x, axis=-1, keepdims=True) + NORM_EPS) * gain


def _dot(a, b):
    return jnp.dot(a, b, preferred_element_type=F32)


def _dot_nt(a, b):
    return lax.dot_general(a, b, (((1,), (1,)), ((), ())), preferred_element_type=F32)


def _dot_tn(a, b):
    return lax.dot_general(a, b, (((0,), (0,)), ((), ())), preferred_element_type=F32)


def _split2(x):
    hi = x.astype(BF16)
    lo = (x - hi.astype(F32)).astype(BF16)
    return hi, lo


def _split3(x):
    hi = x.astype(BF16)
    r1 = x - hi.astype(F32)
    mid = r1.astype(BF16)
    lo = (r1 - mid.astype(F32)).astype(BF16)
    return hi, mid, lo


def _ffn_body(x_ref, gain_ref, wg_ref, wu_ref, wd_ref, fgain_ref, o_ref, *, final_norm):
    x = x_ref[...]
    h = _rms(x, gain_ref[...]).astype(BF16)
    g = _dot(h, wg_ref[...])
    u = _dot(h, wu_ref[...])
    a = (g * jax.nn.sigmoid(g) * u).astype(BF16)
    y = x + 0.5 * _dot(a, wd_ref[...])
    if final_norm:
        y = _rms(y, fgain_ref[...])
    o_ref[...] = y


def _ffn(x, gain, wg, wu, wd, fgain, *, final_norm):
    m, d = x.shape
    f = wg.shape[1]
    tm = FFN_ROWS
    row = pl.BlockSpec((tm, d), lambda i: (i, 0))
    return pl.pallas_call(
        functools.partial(_ffn_body, final_norm=final_norm),
        grid=(m // tm,),
        in_specs=[row, _resident((1, d)), _resident((d, f)), _resident((d, f)),
                  _resident((f, d)), _resident((1, d))],
        out_specs=row,
        out_shape=jax.ShapeDtypeStruct((m, d), F32),
        compiler_params=_params("parallel"),
    )(x, gain, wg, wu, wd, fgain)


def _inproj_body(x_ref, gain_ref, wqkv_ref, wz_ref, wl_ref, wgt_ref, gb_ref, muz_ref,
                 mul_ref, cos_ref, sina_ref, sinb_ref,
                 q_ref, k_ref, v_ref, z_ref, zl_ref, gates_ref, carry_z, carry_l,
                 *, tiles_per_seq):
    tm = x_ref.shape[0]
    h = _rms(x_ref[...], gain_ref[...]).astype(BF16)

    qkv = _dot(h, wqkv_ref[...])
    cos, sina, sinb = cos_ref[...], sina_ref[...], sinb_ref[...]
    half = ROPE_DIM // 2

    def rope(t):
        return (t * cos + pltpu.roll(t, LANES - half, axis=1) * sina
                + pltpu.roll(t, half, axis=1) * sinb)

    for cb in range(ATTN_WIDTH // LANES):
        sl = slice(cb * LANES, (cb + 1) * LANES)
        q_ref[:, sl] = (rope(qkv[:, sl]) * (HEAD_DIM ** -0.5)).astype(BF16)
        ks = slice(ATTN_WIDTH + cb * LANES, ATTN_WIDTH + (cb + 1) * LANES)
        k_ref[:, sl] = rope(qkv[:, ks]).astype(BF16)
    v_ref[...] = qkv[:, 2 * ATTN_WIDTH:].astype(BF16)

    @pl.when((pl.program_id(0) % tiles_per_seq) == 0)
    def _():
        carry_z[...] = jnp.zeros_like(carry_z)
        carry_l[...] = jnp.zeros_like(carry_l)

    row0 = lax.broadcasted_iota(jnp.int32, (tm, 1), 0) == 0

    def shifted(z, carry, mu):
        prev = jnp.where(row0, carry[...], pltpu.roll(z, 1, axis=0))
        carry[...] = z[tm - 1:tm, :]
        return z + (prev - z) * mu

    z_ref[...] = shifted(_dot(h, wz_ref[...]), carry_z, muz_ref[...])
    zl_ref[...] = shifted(_dot(h, wl_ref[...]), carry_l, mul_ref[...])

    gates_ref[...] = jax.nn.sigmoid(_dot(h, wgt_ref[...]) + gb_ref[...])


def _in_proj(x1, gain, wqkv, wz, wl, wgt, gb, muz, mul, cos, sina, sinb, *, seq):
    m, d = x1.shape
    tm = PROJ_ROWS
    tiles_per_seq = seq // tm
    row = lambda w: pl.BlockSpec((tm, w), lambda i: (i, 0))
    table = pl.BlockSpec((tm, LANES), lambda i: (i % tiles_per_seq, 0))
    aw, rw, lw = ATTN_WIDTH, 3 * RWKV_WIDTH, LORA_WIDTH
    return pl.pallas_call(
        functools.partial(_inproj_body, tiles_per_seq=tiles_per_seq),
        grid=(m // tm,),
        in_specs=[row(d), _resident((1, d)), _resident((d, 3 * aw)), _resident((d, rw)),
                  _resident((d, lw)), _resident((d, 2 * d)), _resident((1, 2 * d)),
                  _resident((1, rw)), _resident((1, lw)), table, table, table],
        out_specs=[row(aw), row(aw), row(aw), row(rw), row(lw), row(2 * d)],
        out_shape=[jax.ShapeDtypeStruct((m, aw), BF16)] * 3
        + [jax.ShapeDtypeStruct((m, rw), F32), jax.ShapeDtypeStruct((m, lw), F32),
           jax.ShapeDtypeStruct((m, 2 * d), F32)],
        scratch_shapes=[pltpu.VMEM((1, rw), F32), pltpu.VMEM((1, lw), F32)],
        compiler_params=_params("arbitrary"),
    )(x1, gain, wqkv, wz, wl, wgt, gb, muz, mul, cos, sina, sinb)


def _attn_body(q_ref, kp_ref, kc_ref, vp_ref, vc_ref, acc_ref, m_ref, l_ref):
    n = ATTN_BLOCK
    nq = q_ref.shape[1] // n
    not_first = pl.program_id(3) > 0
    qi = lax.broadcasted_iota(jnp.int32, (n, n), 0)
    ki = lax.broadcasted_iota(jnp.int32, (n, n), 1)
    prev_band = ki >= qi
    cur_band = ki <= qi
    lane = lax.broadcasted_iota(jnp.int32, (n, LANES), 1)
    head0 = lane < HEAD_DIM
    neg_inf = F32(-jnp.inf)

    for j in range(nq):
        rows = slice(j * n, (j + 1) * n)
        q = q_ref[0, rows, :]
        kc, vc = kc_ref[0, rows, :], vc_ref[0, rows, :]
        if j == 0:
            kp, vp = kp_ref[0], vp_ref[0]
            pmask = jnp.logical_and(prev_band, not_first)
        else:
            prow = slice((j - 1) * n, j * n)
            kp, vp = kc_ref[0, prow, :], vc_ref[0, prow, :]
            pmask = prev_band
        accs, ms, ls = [], [], []
        for hd in range(2):
            qh = jnp.where(head0 if hd == 0 else ~head0, q, jnp.zeros_like(q))
            sp = jnp.where(pmask, _dot_nt(qh, kp), neg_inf)
            sc = jnp.where(cur_band, _dot_nt(qh, kc), neg_inf)
            mx = jnp.maximum(jnp.max(sp, axis=-1, keepdims=True),
                             jnp.max(sc, axis=-1, keepdims=True))
            pp = jnp.exp(sp - mx)
            pc = jnp.exp(sc - mx)
            ls.append(jnp.sum(pp, axis=-1, keepdims=True) + jnp.sum(pc, axis=-1, keepdims=True))
            accs.append(_dot(pp.astype(BF16), vp) + _dot(pc.astype(BF16), vc))
            ms.append(mx)
        acc_ref[0, rows, :] = jnp.where(head0, accs[0], accs[1])
        m_ref[0, rows, :] = jnp.where(head0, ms[0], ms[1])
        l_ref[0, rows, :] = jnp.where(head0, ls[0], ls[1])


def _attention_group(q, k, v, group, dilation, *, batch, seq):
    n = ATTN_BLOCK
    sub = seq // dilation
    tq = min(ATTN_ROWS, sub)
    col_blocks = ATTN_WIDTH // LANES
    out_blocks = ATTN_OUT_WIDTH // LANES
    pairs = ATTN_HEADS_PER_GROUP // 2
    view = lambda t: t.reshape(batch, sub, dilation * ATTN_WIDTH)
    qv, kv, vv = view(q), view(k), view(v)
    col = lambda r, p: r * col_blocks + group * pairs + p
    cur = pl.BlockSpec((1, tq, LANES), lambda b, r, p, i: (b, i, col(r, p)))
    prev = pl.BlockSpec((1, n, LANES),
                        lambda b, r, p, i: (b, jnp.maximum(i * (tq // n) - 1, 0), col(r, p)))
    out = pl.BlockSpec((1, tq, LANES), lambda b, r, p, i: (b, i, r * out_blocks + p))
    oshape = jax.ShapeDtypeStruct((batch, sub, dilation * ATTN_OUT_WIDTH), F32)
    acc, m, l = pl.pallas_call(
        _attn_body,
        grid=(batch, dilation, pairs, sub // tq),
        in_specs=[cur, prev, cur, prev, cur],
        out_specs=[out, out, out],
        out_shape=[oshape, oshape, oshape],
        compiler_params=_params("parallel", "parallel", "parallel", "arbitrary"),
    )(qv, kv, kv, vv, vv)
    flat = lambda t: t.reshape(batch * seq, ATTN_OUT_WIDTH)
    return flat(acc), flat(m), flat(l)


def _rwkv_body(r_ref, k_ref, v_ref, zl_ref, w0_ref, w2_ref, a0_ref, a2_ref, g2_ref,
               kk_ref, ka_ref, rk_ref, lnw_ref, lnb_ref, o_ref, state):
    c = RWKV_CHUNK
    ts = r_ref.shape[1]
    nchunks = ts // c

    @pl.when(pl.program_id(2) == 0)
    def _():
        state[...] = jnp.zeros_like(state)

    lane = lax.broadcasted_iota(jnp.int32, (1, LANES), 1)
    head0 = lane < HEAD_DIM
    li = lax.broadcasted_iota(jnp.int32, (LANES, LANES), 0) // HEAD_DIM
    lj = lax.broadcasted_iota(jnp.int32, (LANES, LANES), 1) // HEAD_DIM
    head_ones = jnp.where(li == lj, 1.0, 0.0).astype(BF16)

    def head_sum(x):
        hi, lo = _split2(x)
        return _dot(hi, head_ones) + _dot(lo, head_ones)

    zl = zl_ref[0]
    zw = zl[:, :DECAY_LORA]
    za = zl[:, DECAY_LORA:DECAY_LORA + ICLR_LORA]
    zg = zl[:, DECAY_LORA + ICLR_LORA:]
    wpre = w0_ref[...] + _dot(jnp.tanh(zw).astype(BF16), w2_ref[...])
    logd = -jax.nn.sigmoid(wpre) * F32(0.6065306597126334)
    iclr = jax.nn.sigmoid(a0_ref[...] + _dot(za.astype(BF16), a2_ref[...]))
    gate = _dot(jax.nn.sigmoid(zg).astype(BF16), g2_ref[...])

    r, k, v = r_ref[0], k_ref[0], v_ref[0]
    kk = k * kk_ref[...]
    kk = kk / jnp.maximum(jnp.sqrt(head_sum(kk * kk)), 1e-12)
    k = k * (1.0 + (iclr - 1.0) * ka_ref[...])
    bonus = head_sum(r * k * rk_ref[...]) * v

    ti = lax.broadcasted_iota(jnp.int32, (ts, ts), 0)
    tj = lax.broadcasted_iota(jnp.int32, (ts, ts), 1)
    tri = jnp.where(jnp.logical_and(ti // c == tj // c, tj <= ti), 1.0, 0.0).astype(BF16)
    d_hi, d_mid, d_lo = _split3(logd)
    cum = _dot(tri, d_hi) + _dot(tri, d_mid) + _dot(tri, d_lo)

    p_inc = jnp.exp(cum)
    p_exc = jnp.exp(cum - logd)
    p_inv = jnp.exp(-cum)
    r_s = r * p_inc
    a_s = -kk * p_exc
    b_s = kk * iclr * p_inv
    k_s = k * p_inv

    def stack(x):
        z = jnp.zeros_like(x)
        return jnp.concatenate([jnp.where(head0, x, z), jnp.where(head0, z, x)], axis=0)

    def unstack(x):
        return x[:c] + x[c:]

    si = lax.broadcasted_iota(jnp.int32, (2 * c, 2 * c), 0) % c
    sj = lax.broadcasted_iota(jnp.int32, (2 * c, 2 * c), 1) % c
    strict = sj < si
    lower = sj <= si
    eye = jnp.where(lax.broadcasted_iota(jnp.int32, (2 * c, 2 * c), 0)
                    == lax.broadcasted_iota(jnp.int32, (2 * c, 2 * c), 1), 1.0, 0.0)
    eye_l = jnp.where(lax.broadcasted_iota(jnp.int32, (LANES, LANES), 0)
                      == lax.broadcasted_iota(jnp.int32, (LANES, LANES), 1), 1.0, 0.0)

    h = state[...]
    for ci in range(nchunks):
        rows = slice(ci * c, (ci + 1) * c)
        p_end = p_inc[ci * c + c - 1:ci * c + c, :]
        to_end = p_end * p_inv[rows]
        a_st = stack(a_s[rows]).astype(BF16)
        r_st = stack(r_s[rows])
        b_st = stack(b_s[rows]).astype(BF16)
        k_st = stack(k_s[rows]).astype(BF16)
        v_st = stack(v[rows]).astype(BF16)
        bh_st = stack(kk[rows] * iclr[rows] * to_end).astype(BF16)
        kh_st = stack(k[rows] * to_end).astype(BF16)

        zero = jnp.zeros((2 * c, 2 * c), F32)
        n_ab = jnp.where(strict, _dot_nt(a_st, b_st), zero)
        a_ak = jnp.where(strict, _dot_nt(a_st, k_st), zero).astype(BF16)
        r_bf = r_st.astype(BF16)
        a_rb = jnp.where(lower, _dot_nt(r_bf, b_st), zero).astype(BF16)
        a_rk = jnp.where(lower, _dot_nt(r_bf, k_st), zero).astype(BF16)

        t_inv = eye + n_ab
        npow = n_ab
        steps = c.bit_length() - 1
        for s in range(1, steps):
            nb = npow.astype(BF16)
            npow = _dot(nb, nb)
            t_inv = t_inv + _dot(npow.astype(BF16), t_inv.astype(BF16))
        t_bf = t_inv.astype(BF16)

        a_p = _dot(t_bf, a_st)
        u0 = _dot(t_bf, _dot(a_ak, v_st).astype(BF16))
        a_pb, u0b = a_p.astype(BF16), u0.astype(BF16)
        r_p = r_st + _dot(a_rb, a_pb)
        y0 = _dot(a_rb, u0b) + _dot(a_rk, v_st)
        m_c = eye_l * p_end + _dot_tn(bh_st, a_pb)
        g_c = _dot_tn(bh_st, u0b) + _dot_tn(kh_st, v_st)

        hb = h.astype(BF16)
        y = unstack(_dot(r_p.astype(BF16), hb) + y0)
        h = _dot(m_c.astype(BF16), hb) + g_c

        mu = head_sum(y) * (1.0 / HEAD_DIM)
        yc = y - mu
        var = head_sum(yc * yc) * (1.0 / HEAD_DIM)
        yn = yc * lax.rsqrt(var + RWKV_GN_EPS) * lnw_ref[...] + lnb_ref[...]
        o_ref[0, rows, :] = (yn + bonus[rows]) * gate[rows]
    state[...] = h


def _rwkv(z, zl, w0, w2, a0, a2, g2, k_k, k_a, r_k, ln_w, ln_b, *, batch, seq):
    ts = min(RWKV_ROWS, seq)
    pairs = RWKV_WIDTH // LANES
    z3 = z.reshape(batch, seq, 3 * RWKV_WIDTH)
    zl3 = zl.reshape(batch, seq, LORA_WIDTH)
    stream = lambda off: pl.BlockSpec((1, ts, LANES), lambda b, p, i: (b, i, off * pairs + p))
    vec = pl.BlockSpec((1, LANES), lambda b, p, i: (0, p))
    lora = lambda rows: pl.BlockSpec((rows, LANES), lambda b, p, i: (0, p))
    out = pl.pallas_call(
        _rwkv_body,
        grid=(batch, pairs, seq // ts),
        in_specs=[stream(0), stream(1), stream(2),
                  pl.BlockSpec((1, ts, LORA_WIDTH), lambda b, p, i: (b, i, 0)),
                  vec, lora(DECAY_LORA), vec, lora(ICLR_LORA), lora(GATE_LORA),
                  vec, vec, vec, vec, vec],
        out_specs=pl.BlockSpec((1, ts, LANES), lambda b, p, i: (b, i, p)),
        out_shape=jax.ShapeDtypeStruct((batch, seq, RWKV_WIDTH), F32),
        scratch_shapes=[pltpu.VMEM((LANES, LANES), F32)],
        compiler_params=_params("parallel", "parallel", "arbitrary"),
    )(z3, z3, z3, zl3, w0, w2, a0, a2, g2, k_k, k_a, r_k, ln_w, ln_b)
    return out.reshape(batch * seq, RWKV_WIDTH)


def _merge_body(*refs):
    (acc0, m0, l0, acc1, m1, l1, acc2, m2, l2, yg_ref, gates_ref, x_ref,
     wup_ref, wout_ref, wo_ref, o_ref) = refs
    ms = [m0[...], m1[...], m2[...]]
    mmax = jnp.maximum(jnp.maximum(ms[0], ms[1]), ms[2])
    cs = [jnp.exp(m - mmax) for m in ms]
    num = cs[0] * acc0[...] + cs[1] * acc1[...] + cs[2] * acc2[...]
    den = cs[0] * l0[...] + cs[1] * l1[...] + cs[2] * l2[...]
    att = (num / den).astype(BF16)
    y_attn = _dot(att, wup_ref[...])
    y_rwkv = _dot(yg_ref[...].astype(BF16), wout_ref[...])
    d = x_ref.shape[1]
    merged = gates_ref[:, :d] * y_attn + gates_ref[:, d:] * y_rwkv
    o_ref[...] = x_ref[...] + _dot(merged.astype(BF16), wo_ref[...])


def _merge_out(attn_parts, yg, gates, x1, wup, wout, wo):
    m, d = x1.shape
    tm = MERGE_ROWS
    row = lambda w: pl.BlockSpec((tm, w), lambda i: (i, 0))
    flat = [t for part in attn_parts for t in part]
    return pl.pallas_call(
        _merge_body,
        grid=(m // tm,),
        in_specs=[row(ATTN_OUT_WIDTH)] * 9 + [row(d), row(2 * d), row(d),
                                              _resident(wup.shape), _resident(wout.shape),
                                              _resident(wo.shape)],
        out_specs=row(d),
        out_shape=jax.ShapeDtypeStruct((m, d), F32),
        compiler_params=_params("parallel"),
    )(*flat, yg, gates, x1, wup, wout, wo)


def _rope_tables(seq):
    half = ROPE_DIM // 2
    inv_freq = jnp.power(ROPE_THETA, -jnp.arange(half, dtype=F32) * (2.0 / ROPE_DIM))
    ang = jnp.arange(seq, dtype=jnp.int32).astype(F32)[:, None] * inv_freq[None, :]
    cos, sin = jnp.cos(ang), jnp.sin(ang)
    ones = jnp.ones((seq, HEAD_DIM - ROPE_DIM), F32)
    zeros = jnp.zeros((seq, HEAD_DIM - ROPE_DIM), F32)
    zh = jnp.zeros((seq, half), F32)
    cos_t = jnp.concatenate([cos, cos, ones], axis=1)
    sina_t = jnp.concatenate([-sin, zh, zeros], axis=1)
    sinb_t = jnp.concatenate([zh, sin, zeros], axis=1)
    rep = LANES // HEAD_DIM
    return tuple(jnp.tile(t, (1, rep)) for t in (cos_t, sina_t, sinb_t))


def kernel(x, ffn1_norm, ffn1_w_gate, ffn1_w_up, ffn1_w_down, mix_norm, w_in, gate_bias,
           attn_w_up, rwkv_mu, rwkv_w0, rwkv_w2, rwkv_a0, rwkv_a2, rwkv_g2, rwkv_k_k,
           rwkv_k_a, rwkv_r_k, rwkv_ln_w, rwkv_ln_b, rwkv_w_out, w_o, ffn2_norm,
           ffn2_w_gate, ffn2_w_up, ffn2_w_down, final_norm):
    batch, seq, d = x.shape
    depth = ffn1_norm.shape[0]
    aw, rw = ATTN_WIDTH, 3 * RWKV_WIDTH
    bf = lambda t: t.astype(BF16)
    vec = lambda t: t.reshape(1, -1)
    cos_t, sina_t, sinb_t = _rope_tables(seq)
    fgain = vec(final_norm)

    xf = x.reshape(batch * seq, d)
    for layer in range(depth):
        x1 = _ffn(xf, vec(ffn1_norm[layer]), bf(ffn1_w_gate[layer]), bf(ffn1_w_up[layer]),
                  bf(ffn1_w_down[layer]), fgain, final_norm=False)
        w = w_in[layer]
        mu = rwkv_mu[layer]
        q, k, v, z, zl, gates = _in_proj(
            x1, vec(mix_norm[layer]), bf(w[:, :3 * aw]), bf(w[:, 3 * aw:3 * aw + rw]),
            bf(w[:, 3 * aw + rw:3 * aw + RWKV_STREAM]), bf(w[:, 3 * aw + RWKV_STREAM:]),
            vec(gate_bias[layer]), vec(mu[:rw]), vec(mu[rw:]), cos_t, sina_t, sinb_t, seq=seq)
        attn_parts = [
            _attention_group(q, k, v, g, dilation, batch=batch, seq=seq)
            for g, (_, dilation) in enumerate(ATTN_GROUPS)]
        yg = _rwkv(z, zl, vec(rwkv_w0[layer]), bf(rwkv_w2[layer]), vec(rwkv_a0[layer]),
                   bf(rwkv_a2[layer]), bf(rwkv_g2[layer]), vec(rwkv_k_k[layer]),
                   vec(rwkv_k_a[layer]), vec(rwkv_r_k[layer]), vec(rwkv_ln_w[layer]),
                   vec(rwkv_ln_b[layer]), batch=batch, seq=seq)
        x2 = _merge_out(attn_parts, yg, gates, x1, bf(attn_w_up[layer]),
                        bf(rwkv_w_out[layer]), bf(w_o[layer]))
        xf = _ffn(x2, vec(ffn2_norm[layer]), bf(ffn2_w_gate[layer]), bf(ffn2_w_up[layer]),
                  bf(ffn2_w_down[layer]), fgain, final_norm=(layer == depth - 1))
    return xf.reshape(batch, seq, d)
```
